```python
import functools
import jax
import jax.numpy as jnp
from jax import lax
import numpy as np

D_MODEL = 2048
BATCH = 1
SEQ = 16384
DEPTH = 1
DEC_BATCH = 32
DEC_SEQ = 8
PAST_LEN = 16384
PAGE_SIZE = 128

HEAD_DIM = 128
NSA_HEADS = 8
NSA_KV_HEADS = 2
NSA_REP = NSA_HEADS // NSA_KV_HEADS
NSA_WIDTH = NSA_HEADS * HEAD_DIM
NSA_KV_WIDTH = 3 * 2 * NSA_KV_HEADS * HEAD_DIM
NSA_GATE_WIDTH = 3 * NSA_HEADS
CMP_BLOCK = 32
CMP_STRIDE = 16
CMP_HIDDEN = 256
SLC_BLOCK = 64
N_SLC = 16
WINDOW = 512
Q_BLOCK = 128

GDN_HEADS = 8
GDN_DK = 128
GDN_DV = 128
GDN_WIDTH = GDN_HEADS * GDN_DV
GDN_QKV_WIDTH = 2 * GDN_HEADS * GDN_DK + GDN_WIDTH
CONV_W = 4
GDN_CHUNK = 64

MEM_TOKENS = 256
MEM_HEADS = 4
MEM_HD = D_MODEL // MEM_HEADS

N_GROUPS = 4
EXPERTS_PER_GROUP = 8
N_EXPERTS = N_GROUPS * EXPERTS_PER_GROUP
TOP_K = 2
D_FF_EXPERT = 512
MOE_BLOCK = 128

IN_SIZES = (NSA_WIDTH, NSA_KV_WIDTH, NSA_GATE_WIDTH, GDN_QKV_WIDTH, GDN_HEADS, GDN_HEADS, GDN_WIDTH, 2 * D_MODEL)
IN_WIDTH = sum(IN_SIZES)
RMS_EPS = 1e-6

kernel_name = 'nsa_gdn_memxattn_hmoe_decode_step'


def rmsnorm(x, g):
    xf = x.astype(jnp.float32)
    y = xf * lax.rsqrt(jnp.mean(xf * xf, axis=-1, keepdims=True) + RMS_EPS)
    return (y * g.astype(jnp.float32)).astype(x.dtype)


def l2norm(x):
    xf = x.astype(jnp.float32)
    return xf * lax.rsqrt(jnp.sum(xf * xf, axis=-1, keepdims=True) + 1e-6)


def masked_softmax(s, mask):
    s = jnp.where(mask, s.astype(jnp.float32), -jnp.inf)
    m = jnp.max(s, axis=-1, keepdims=True)
    m = jnp.where(jnp.isfinite(m), m, 0.0)
    e = jnp.where(mask, jnp.exp(s - m), 0.0)
    return e / jnp.maximum(jnp.sum(e, axis=-1, keepdims=True), 1e-30)


def mixer_inputs(h, w_in):
    B, T, _ = h.shape
    cuts = np.cumsum(IN_SIZES)[:-1].tolist()
    q, kv, g3, qkv, a, b, z, bg = jnp.split(h @ w_in, cuts, axis=-1)
    q = q.reshape(B, T, NSA_KV_HEADS, NSA_REP, HEAD_DIM)
    kv = kv.reshape(B, T, 3, NSA_KV_HEADS, 2, HEAD_DIM)
    g3 = jax.nn.sigmoid(g3.reshape(B, T, NSA_KV_HEADS, NSA_REP, 3))
    return q, kv, g3, qkv, a, b, z, bg


def nsa_compress(k, pe, w1, b1, w2):
    B, L, G, D = k.shape
    n = L // CMP_STRIDE
    r_blk = CMP_BLOCK // CMP_STRIDE
    nb = n - r_blk + 1
    ch = k.reshape(B, n, CMP_STRIDE, G, D)
    pe = pe.reshape(r_blk, CMP_STRIDE, D)
    w1 = w1.reshape(r_blk, CMP_STRIDE, D, -1)
    h = b1
    for r in range(r_blk):
        h = h + jnp.einsum('bnsgd,sdh->bngh', ch[:, r:r + nb], w1[r]) + jnp.einsum('sd,sdh->h', pe[r], w1[r])
    return jnp.einsum('bngh,hd->bngd', jax.nn.gelu(h), w2)


def compress_kv(kv_cmp, cmp_w):
    pe, w1, b1, w2 = cmp_w
    kc = nsa_compress(kv_cmp[..., 0, :], pe[0], w1[0], b1[0], w2[0])
    vc = nsa_compress(kv_cmp[..., 1, :], pe[1], w1[1], b1[1], w2[1])
    return kc, vc


def selection_scores(p, n_sel):
    r_blk = CMP_BLOCK // CMP_STRIDE
    per_sel = SLC_BLOCK // CMP_STRIDE
    n_chunks = n_sel * per_sel
    nc = p.shape[-1]
    pp = jnp.pad(p, [(0, 0)] * (p.ndim - 1) + [(r_blk - 1, max(0, n_chunks - nc))])
    u = pp[..., r_blk - 1:r_blk - 1 + n_chunks]
    for rr in range(1, r_blk):
        u = u + pp[..., r_blk - 1 - rr:r_blk - 1 - rr + n_chunks]
    return u.reshape(p.shape[:-1] + (n_sel, per_sel)).sum(-1)


def select_blocks(imp, q_pos):
    n_sel = imp.shape[-1]
    j = jnp.arange(n_sel)[None, :]
    cur = (q_pos // SLC_BLOCK)[:, None]
    causal = j * SLC_BLOCK <= q_pos[:, None]
    forced = (j == 0) | (j == cur) | (j == cur - 1)
    s = jnp.where(forced[None, :, None, :], jnp.inf, imp)
    s = jnp.where(causal[None, :, None, :], s, -jnp.inf)
    top_s, top_j = lax.top_k(s, min(N_SLC, n_sel))
    return top_j, top_s > -jnp.inf


def nsa_attend(q, gates, q_pos, kc, vc, c_end, sel_rows, kvw, w_pos, seq_len):
    scale = HEAD_DIM ** -0.5
    cmask = (c_end[None, :] <= q_pos[:, None])[None, :, None, None, :]
    p_c = masked_softmax(jnp.einsum('btgrd,bngd->btgrn', q, kc) * scale, cmask)
    o_c = jnp.einsum('btgrn,bngd->btgrd', p_c.astype(vc.dtype), vc)
    imp = selection_scores(jnp.sum(p_c, axis=3), -(-seq_len // SLC_BLOCK))
    top_j, valid = select_blocks(imp, q_pos)
    B, T, G = top_j.shape[:3]
    pos = top_j[..., None] * SLC_BLOCK + jnp.arange(SLC_BLOCK)
    smask = (valid[..., None] & (pos <= q_pos[None, :, None, None, None])).reshape(B, T, G, -1)
    rows = sel_rows(jnp.minimum(pos.reshape(B, T, G, -1), seq_len - 1))
    p_s = masked_softmax(jnp.einsum('btgrd,btgnd->btgrn', q, rows[..., 0, :]) * scale, smask[:, :, :, None, :])
    o_s = jnp.einsum('btgrn,btgnd->btgrd', p_s.astype(rows.dtype), rows[..., 1, :])
    dist = q_pos[:, None] - w_pos[None, :]
    wmask = ((dist >= 0) & (dist < WINDOW) & (w_pos[None, :] >= 0))[None, :, None, None, :]
    p_w = masked_softmax(jnp.einsum('btgrd,bwgd->btgrw', q, kvw[..., 0, :]) * scale, wmask)
    o_w = jnp.einsum('btgrw,bwgd->btgrd', p_w.astype(kvw.dtype), kvw[..., 1, :])
    return gates[..., 0:1] * o_c + gates[..., 1:2] * o_s + gates[..., 2:3] * o_w


def nsa_prompt(q, gates, kv, cmp_w):
    B, S = q.shape[:2]
    kc, vc = compress_kv(kv[:, :, 0], cmp_w)
    c_end = jnp.arange(kc.shape[1]) * CMP_STRIDE + CMP_BLOCK - 1
    kv_slc = kv[:, :, 1]
    kvw_pad = jnp.pad(kv[:, :, 2], ((0, 0), (WINDOW, 0), (0, 0), (0, 0), (0, 0)))
    bidx = jnp.arange(B)[:, None, None, None]
    gidx = jnp.arange(NSA_KV_HEADS)[None, None, :, None]

    def sel_rows(pos):
        return kv_slc[bidx, pos, gidx]

    def block(i):
        st = i * Q_BLOCK
        q_pos = st + jnp.arange(Q_BLOCK)
        w_pos = st - WINDOW + jnp.arange(WINDOW + Q_BLOCK)
        return nsa_attend(lax.dynamic_slice_in_dim(q, st, Q_BLOCK, axis=1),
                          lax.dynamic_slice_in_dim(gates, st, Q_BLOCK, axis=1),
                          q_pos, kc, vc, c_end, sel_rows,
                          lax.dynamic_slice_in_dim(kvw_pad, st, WINDOW + Q_BLOCK, axis=1), w_pos, S)

    o = lax.map(block, jnp.arange(S // Q_BLOCK))
    o = jnp.moveaxis(o, 0, 1).reshape(B, S, NSA_WIDTH)
    return o, kv[:, S - min(WINDOW, S):, 2]


def nsa_sample(q, gates, kv, cache_cmp, cache_slc, win_buf, page_table, cmp_w):
    Bd, T = q.shape[:2]
    n_pages = page_table.shape[1]
    past = n_pages * PAGE_SIZE
    L = past + T
    Lp = -(-L // CMP_STRIDE) * CMP_STRIDE
    n_cmp = Lp // CMP_STRIDE - CMP_BLOCK // CMP_STRIDE + 1
    c_end = jnp.arange(n_cmp) * CMP_STRIDE + CMP_BLOCK - 1
    q_pos = past + jnp.arange(T)
    wb = win_buf.shape[1]
    w_pos = past - wb + jnp.arange(wb + T)
    gidx = jnp.arange(NSA_KV_HEADS)[None, None, :, None]

    def one(args):
        qs, gs, kvs, pt, wbuf = args
        past_cmp = cache_cmp[pt].reshape(past, NSA_KV_HEADS, 2, HEAD_DIM)
        full_cmp = jnp.concatenate([past_cmp, kvs[:, 0].astype(past_cmp.dtype),
                                    jnp.zeros((Lp - L, NSA_KV_HEADS, 2, HEAD_DIM), past_cmp.dtype)], axis=0)
        kc, vc = compress_kv(full_cmp[None], cmp_w)

        def sel_rows(pos):
            phys = pt[jnp.minimum(pos // PAGE_SIZE, n_pages - 1)]
            from_pool = cache_slc[phys, pos % PAGE_SIZE, gidx]
            from_new = kvs[:, 1][jnp.clip(pos - past, 0, T - 1), gidx]
            return jnp.where((pos < past)[..., None, None], from_pool, from_new.astype(from_pool.dtype))

        kvw = jnp.concatenate([wbuf, kvs[:, 2].astype(wbuf.dtype)], axis=0)[None]
        return nsa_attend(qs[None], gs[None], q_pos, kc, vc, c_end, sel_rows, kvw, w_pos, L)[0]

    o = lax.map(one, (q, gates, kv, page_table, win_buf))
    new_win = jnp.concatenate([win_buf, kv[:, :, 2].astype(win_buf.dtype)], axis=1)[:, -wb:]
    return o.reshape(Bd, T, NSA_WIDTH), new_win


def causal_conv(x, buf, w):
    T = x.shape[1]
    xp = jnp.concatenate([buf.astype(x.dtype), x], axis=1)
    y = xp[:, 0:T] * w[0]
    for i in range(1, CONV_W):
        y = y + xp[:, i:i + T] * w[i]
    return jax.nn.silu(y), xp[:, T:]


def gated_delta_chunked(q, k, v, g, beta, s0):
    B, T, H, DK = q.shape
    DV = v.shape[-1]
    C = GDN_CHUNK
    Tp = -(-T // C) * C
    n = Tp // C

    def chunks(a):
        a = jnp.pad(a, [(0, 0), (0, Tp - T)] + [(0, 0)] * (a.ndim - 2))
        a = a.reshape((B, n, C, H) + a.shape[3:])
        return jnp.moveaxis(a, (1, 3), (0, 2))

    q = chunks(q) * DK ** -0.5
    k = chunks(k)
    v = chunks(v)
    g = chunks(g)
    beta = chunks(beta)
    G = jnp.cumsum(g, axis=-1)
    idx = jnp.arange(C)
    causal = idx[:, None] >= idx[None, :]
    strict = idx[:, None] > idx[None, :]
    decay = jnp.exp(jnp.where(causal, G[..., :, None] - G[..., None, :], -jnp.inf))
    kb = k * beta[..., None]
    A = jnp.where(strict, jnp.einsum('nbhid,nbhjd->nbhij', kb, k) * decay, 0.0)
    eye = jnp.eye(C, dtype=A.dtype)
    t_inv = lax.linalg.triangular_solve(eye + A, jnp.broadcast_to(eye, A.shape), left_side=True, lower=True)
    w = t_inv @ (kb * jnp.exp(G)[..., None])
    u = t_inv @ (v * beta[..., None])
    qk = jnp.einsum('nbhid,nbhjd->nbhij', q, k) * decay

    def step(S, xs):
        qc, kc, wc, uc, qkc, Gc = xs
        v_new = uc - wc @ S
        o = (qc * jnp.exp(Gc)[..., None]) @ S + qkc @ v_new
        g_last = Gc[..., -1:]
        S = S * jnp.exp(g_last)[..., None] + jnp.einsum('bhcd,bhce->bhde', kc * jnp.exp(g_last - Gc)[..., None], v_new)
        return S, o

    S, o = lax.scan(step, s0, (q, k, w, u, qk, G))
    o = jnp.moveaxis(o, (0, 2), (1, 3)).reshape(B, Tp, H, DV)[:, :T]
    return o, S


def gdn_branch(qkv, a, b, z, conv_buf, s0, conv_w, a_log, dt_bias, norm_g):
    B, T, _ = qkv.shape
    xc, new_buf = causal_conv(qkv, conv_buf, conv_w)
    qc, kc, vc = jnp.split(xc, [GDN_HEADS * GDN_DK, 2 * GDN_HEADS * GDN_DK], axis=-1)
    q = l2norm(qc.reshape(B, T, GDN_HEADS, GDN_DK))
    k = l2norm(kc.reshape(B, T, GDN_HEADS, GDN_DK))
    v = vc.reshape(B, T, GDN_HEADS, GDN_DV).astype(jnp.float32)
    g = -jnp.exp(a_log.astype(jnp.float32)) * jax.nn.softplus(a.astype(jnp.float32) + dt_bias.astype(jnp.float32))
    beta = jax.nn.sigmoid(b.astype(jnp.float32))
    o, s = gated_delta_chunked(q, k, v, g, beta, s0.astype(jnp.float32))
    o = rmsnorm(o, norm_g) * jax.nn.silu(z.reshape(B, T, GDN_HEADS, GDN_DV).astype(jnp.float32))
    return o.reshape(B, T, GDN_WIDTH).astype(qkv.dtype), s.astype(s0.dtype), new_buf


def memory_kv(mem, g, w_mk, w_mv):
    B, M, _ = mem.shape
    m = rmsnorm(mem, g)
    k = (m @ w_mk).reshape(B, M, MEM_HEADS, MEM_HD)
    v = (m @ w_mv).reshape(B, M, MEM_HEADS, MEM_HD)
    return jnp.stack([k, v], axis=2)


def memory_attend(h, kv, w_mq, w_mo):
    B, T, _ = h.shape
    q = (h @ w_mq).reshape(B, T, MEM_HEADS, MEM_HD)
    s = jnp.einsum('bthd,bmhd->bhtm', q, kv[:, :, 0]).astype(jnp.float32) * MEM_HD ** -0.5
    p = jax.nn.softmax(s, axis=-1).astype(h.dtype)
    o = jnp.einsum('bhtm,bmhd->bthd', p, kv[:, :, 1])
    return o.reshape(B, T, D_MODEL).astype(h.dtype) @ w_mo


def moe_dispatch(x, eid, wts, w_gate, w_up, w_down):
    N, D = x.shape
    A = N * TOP_K
    e_flat = eid.reshape(-1)
    tok = jnp.repeat(jnp.arange(N), TOP_K)
    w_flat = wts.reshape(-1)
    order = jnp.argsort(e_flat)
    e_s = e_flat[order]
    counts = jnp.bincount(e_flat, length=N_EXPERTS)
    padded = (counts + MOE_BLOCK - 1) // MOE_BLOCK * MOE_BLOCK
    pad_end = jnp.cumsum(padded)
    pad_start = pad_end - padded
    start = jnp.cumsum(counts) - counts
    dest = pad_start[e_s] + jnp.arange(A) - start[e_s]
    n_blocks = (A + N_EXPERTS * (MOE_BLOCK - 1) + MOE_BLOCK - 1) // MOE_BLOCK
    P = n_blocks * MOE_BLOCK
    slot_tok = jnp.full((P,), N, jnp.int32).at[dest].set(tok[order].astype(jnp.int32))
    slot_w = jnp.zeros((P,), jnp.float32).at[dest].set(w_flat[order])
    block_e = jnp.minimum(jnp.searchsorted(pad_end, jnp.arange(n_blocks) * MOE_BLOCK, side='right'), N_EXPERTS - 1)
    x_pad = jnp.concatenate([x, jnp.zeros((1, D), x.dtype)], axis=0)
    xb = x_pad[slot_tok].reshape(n_blocks, MOE_BLOCK, D)

    def expert_block(args):
        xe, e = args
        hdn = jax.nn.silu(xe @ w_gate[e]) * (xe @ w_up[e])
        return hdn @ w_down[e]

    yb = lax.map(expert_block, (xb, block_e)).reshape(P, D)
    y = jax.ops.segment_sum(yb * slot_w[:, None].astype(yb.dtype), slot_tok, num_segments=N + 1)
    return y[:N]


def hier_moe(h, w_grp, b_grp, w_exp, b_exp, w_gate, w_up, w_down):
    B, T, D = h.shape
    n_tok = B * T
    x = h.reshape(n_tok, D)
    p_grp = jax.nn.softmax((x @ w_grp).astype(jnp.float32) + b_grp.astype(jnp.float32), axis=-1)
    top_pg, top_g = lax.top_k(p_grp, 1)
    le = ((x @ w_exp).astype(jnp.float32) + b_exp.astype(jnp.float32)).reshape(n_tok, N_GROUPS, EXPERTS_PER_GROUP)
    p_in = jax.nn.softmax(jnp.take_along_axis(le, top_g[:, :, None], axis=1)[:, 0], axis=-1)
    top_pe, top_e = lax.top_k(p_in, TOP_K)
    wts = top_pg * top_pe / jnp.sum(top_pe, axis=-1, keepdims=True)
    eid = top_g * EXPERTS_PER_GROUP + top_e
    return moe_dispatch(x, eid, wts, w_gate, w_up, w_down).reshape(B, T, D).astype(h.dtype)


def layer_forward(x, nsa_fn, conv_buf, gdn_s0, mem_rows, norm_mix, w_in, gdn_conv_w, gdn_a_log, gdn_dt_bias,
                  gdn_norm, w_up_nsa, w_up_gdn, w_out, norm_mem, w_mq, w_mo, norm_ffn, w_grp, b_grp, w_exp, b_exp,
                  w_gate, w_up, w_down):
    h = rmsnorm(x, norm_mix)
    q, kv, g3, qkv, a, b, z, bg = mixer_inputs(h, w_in)
    o_nsa, win_state = nsa_fn(q, g3, kv)
    o_gdn, s_new, conv_new = gdn_branch(qkv, a, b, z, conv_buf, gdn_s0, gdn_conv_w, gdn_a_log, gdn_dt_bias, gdn_norm)
    g_nsa, g_gdn = jnp.split(jax.nn.sigmoid(bg), 2, axis=-1)
    x = x + (g_nsa * (o_nsa @ w_up_nsa) + g_gdn * (o_gdn @ w_up_gdn)) @ w_out
    x = x + memory_attend(rmsnorm(x, norm_mem), mem_rows, w_mq, w_mo)
    x = x + hier_moe(rmsnorm(x, norm_ffn), w_grp, b_grp, w_exp, b_exp, w_gate, w_up, w_down)
    return x, kv, win_state, s_new, conv_new


def setup_inputs(seed: int = 0) -> dict:
    key = jax.random.key(seed)
    ks = iter(jax.random.split(key, 48))
    f32 = jnp.float32

    def nrm(shape, scale=1.0):
        return jax.random.normal(next(ks), shape, f32) * scale

    def gain(shape):
        return 1.0 + 0.01 * jax.random.normal(next(ks), shape, f32)

    n_pages = PAST_LEN // PAGE_SIZE
    n_used = DEC_BATCH * n_pages
    n_phys = n_used + (n_used + 3) // 4
    win_buf = min(WINDOW, PAST_LEN)
    page_table = jax.random.permutation(next(ks), n_phys)[:n_used].reshape(DEC_BATCH, n_pages).astype(jnp.int32)
    dt = jax.random.uniform(next(ks), (DEPTH, GDN_HEADS), f32, 1e-3, 0.1)
    return {
        'x_prompt': nrm((BATCH, SEQ, D_MODEL)),
        'x_sample': nrm((DEC_BATCH, DEC_SEQ, D_MODEL)),
        'cache_cmp_kv': nrm((DEPTH, n_phys, PAGE_SIZE, NSA_KV_HEADS, 2, HEAD_DIM)),
        'cache_slc_kv': nrm((DEPTH, n_phys, PAGE_SIZE, NSA_KV_HEADS, 2, HEAD_DIM)),
        'cache_win_kv': nrm((DEPTH, DEC_BATCH, win_buf, NSA_KV_HEADS, 2, HEAD_DIM)),
        'state_gdn': nrm((DEPTH, DEC_BATCH, GDN_HEADS, GDN_DK, GDN_DV), 0.1),
        'state_gdn_conv': nrm((DEPTH, DEC_BATCH, CONV_W - 1, GDN_QKV_WIDTH)),
        'cache_mem_kv': nrm((DEPTH, DEC_BATCH, MEM_TOKENS, 2, MEM_HEADS, MEM_HD)),
        'page_table': page_table,
        'mem_prompt': nrm((BATCH, MEM_TOKENS, D_MODEL)),
        'norm_mix': gain((DEPTH, D_MODEL)),
        'w_in': nrm((DEPTH, D_MODEL, IN_WIDTH), D_MODEL ** -0.5),
        'cmp_pe': nrm((DEPTH, 2, CMP_BLOCK, HEAD_DIM), 0.02),
        'cmp_w1': nrm((DEPTH, 2, CMP_BLOCK, HEAD_DIM, CMP_HIDDEN), (CMP_BLOCK * HEAD_DIM) ** -0.5),
        'cmp_b1': nrm((DEPTH, 2, CMP_HIDDEN), 0.01),
        'cmp_w2': nrm((DEPTH, 2, CMP_HIDDEN, HEAD_DIM), CMP_HIDDEN ** -0.5),
        'gdn_conv_w': nrm((DEPTH, CONV_W, GDN_QKV_WIDTH), CONV_W ** -0.5),
        'gdn_a_log': jnp.log(jax.random.uniform(next(ks), (DEPTH, GDN_HEADS), f32, 1.0, 16.0)),
        'gdn_dt_bias': jnp.log(jnp.expm1(dt)),
        'gdn_norm': gain((DEPTH, GDN_DV)),
        'w_up_nsa': nrm((DEPTH, NSA_WIDTH, D_MODEL), NSA_WIDTH ** -0.5),
        'w_up_gdn': nrm((DEPTH, GDN_WIDTH, D_MODEL), GDN_WIDTH ** -0.5),
        'w_out': nrm((DEPTH, D_MODEL, D_MODEL), D_MODEL ** -0.5),
        'norm_mem': gain((DEPTH, D_MODEL)),
        'mem_norm': gain((DEPTH, D_MODEL)),
        'w_mq': nrm((DEPTH, D_MODEL, D_MODEL), D_MODEL ** -0.5),
        'w_mk': nrm((DEPTH, D_MODEL, D_MODEL), D_MODEL ** -0.5),
        'w_mv': nrm((DEPTH, D_MODEL, D_MODEL), D_MODEL ** -0.5),
        'w_mo': nrm((DEPTH, D_MODEL, D_MODEL), D_MODEL ** -0.5),
        'norm_ffn': gain((DEPTH, D_MODEL)),
        'w_grp': nrm((DEPTH, D_MODEL, N_GROUPS), D_MODEL ** -0.5),
        'b_grp': nrm((DEPTH, N_GROUPS), 0.01),
        'w_exp': nrm((DEPTH, D_MODEL, N_EXPERTS), D_MODEL ** -0.5),
        'b_exp': nrm((DEPTH, N_EXPERTS), 0.01),
        'w_gate': nrm((DEPTH, N_EXPERTS, D_MODEL, D_FF_EXPERT), D_MODEL ** -0.5),
        'w_up': nrm((DEPTH, N_EXPERTS, D_MODEL, D_FF_EXPERT), D_MODEL ** -0.5),
        'w_down': nrm((DEPTH, N_EXPERTS, D_FF_EXPERT, D_MODEL), D_FF_EXPERT ** -0.5),
        'final_norm': gain((D_MODEL,)),
    }


def reference(x_prompt, x_sample, cache_cmp_kv, cache_slc_kv, cache_win_kv, state_gdn, state_gdn_conv, cache_mem_kv,
              page_table, mem_prompt, norm_mix, w_in, cmp_pe, cmp_w1, cmp_b1, cmp_w2, gdn_conv_w, gdn_a_log,
              gdn_dt_bias, gdn_norm, w_up_nsa, w_up_gdn, w_out, norm_mem, mem_norm, w_mq, w_mk, w_mv, w_mo,
              norm_ffn, w_grp, b_grp, w_exp, b_exp, w_gate, w_up, w_down, final_norm):
    xp, xs = x_prompt, x_sample
    B = xp.shape[0]
    cmp_p, slc_p, win_p, gdn_p, conv_p, mem_p = [], [], [], [], [], []
    cmp_s, slc_s, win_s, gdn_s, conv_s = [], [], [], [], []
    for l in range(DEPTH):
        layer_w = (norm_mix[l], w_in[l], gdn_conv_w[l], gdn_a_log[l], gdn_dt_bias[l], gdn_norm[l], w_up_nsa[l],
                   w_up_gdn[l], w_out[l], norm_mem[l], w_mq[l], w_mo[l], norm_ffn[l], w_grp[l], b_grp[l], w_exp[l],
                   b_exp[l], w_gate[l], w_up[l], w_down[l])
        cmp_w = (cmp_pe[l], cmp_w1[l], cmp_b1[l], cmp_w2[l])
        mem_rows = memory_kv(mem_prompt, mem_norm[l], w_mk[l], w_mv[l])
        nsa_p = functools.partial(nsa_prompt, cmp_w=cmp_w)
        xp, kv, win, s_new, conv_new = layer_forward(
            xp, nsa_p, jnp.zeros((B, CONV_W - 1, GDN_QKV_WIDTH), xp.dtype),
            jnp.zeros((B, GDN_HEADS, GDN_DK, GDN_DV), xp.dtype), mem_rows, *layer_w)
        cmp_p.append(kv[:, :, 0])
        slc_p.append(kv[:, :, 1])
        win_p.append(win)
        gdn_p.append(s_new)
        conv_p.append(conv_new)
        mem_p.append(mem_rows)
        nsa_s = functools.partial(nsa_sample, cache_cmp=cache_cmp_kv[l], cache_slc=cache_slc_kv[l],
                                  win_buf=cache_win_kv[l], page_table=page_table, cmp_w=cmp_w)
        xs, kv, win, s_new, conv_new = layer_forward(xs, nsa_s, state_gdn_conv[l], state_gdn[l], cache_mem_kv[l], *layer_w)
        cmp_s.append(kv[:, :, 0])
        slc_s.append(kv[:, :, 1])
        win_s.append(win)
        gdn_s.append(s_new)
        conv_s.append(conv_new)
    y_prompt = rmsnorm(xp, final_norm)
    y_sample = rmsnorm(xs, final_norm)
    new_cmp_kv_prompt = jnp.stack(cmp_p)
    new_slc_kv_prompt = jnp.stack(slc_p)
    new_win_kv_prompt = jnp.stack(win_p)
    new_gdn_state_prompt = jnp.stack(gdn_p)
    new_gdn_conv_prompt = jnp.stack(conv_p)
    new_mem_kv_prompt = jnp.stack(mem_p)
    new_cmp_kv_sample = jnp.stack(cmp_s)
    new_slc_kv_sample = jnp.stack(slc_s)
    new_win_kv_sample = jnp.stack(win_s)
    new_gdn_state_sample = jnp.stack(gdn_s)
    new_gdn_conv_sample = jnp.stack(conv_s)
    return (y_prompt, y_sample, new_cmp_kv_prompt, new_slc_kv_prompt, new_win_kv_prompt, new_gdn_state_prompt,
            new_gdn_conv_prompt, new_mem_kv_prompt, new_cmp_kv_sample, new_slc_kv_sample, new_win_kv_sample,
            new_gdn_state_sample, new_gdn_conv_sample)
```

```python
import functools

import numpy as np
import jax
import jax.numpy as jnp
from jax import lax
from jax.experimental import pallas as pl
from jax.experimental.pallas import tpu as pltpu

F32 = jnp.float32
BF16 = jnp.bfloat16
HI = lax.Precision.HIGHEST

V7X_VMEM_LIMIT_BYTES = 56 * 1024 * 1024
LANES = 128
SUBLANES = 8

HEAD_DIM = 128
NSA_HEADS = 8
NSA_KV_HEADS = 2
NSA_REP = NSA_HEADS // NSA_KV_HEADS
CMP_BLOCK = 32
CMP_STRIDE = 16
CMP_HIDDEN = 256
SLC_BLOCK = 64
N_SLC = 16
WINDOW = 512
Q_BLOCK = 128
PAGE_SIZE = 128
GDN_HEADS = 8
GDN_DK = 128
GDN_DV = 128
CONV_W = 4
GDN_CHUNK = 64
MEM_HEADS = 4
N_GROUPS = 4
EXPERTS_PER_GROUP = 8
N_EXPERTS = N_GROUPS * EXPERTS_PER_GROUP
TOP_K = 2
RMS_EPS = 1e-6
NEG_BIG = -1e30

NSA_WIDTH = NSA_HEADS * HEAD_DIM
KV_BRANCH = 2 * NSA_KV_HEADS * HEAD_DIM
GDN_WIDTH = GDN_HEADS * GDN_DV
GDN_QKV = 2 * GDN_HEADS * GDN_DK + GDN_WIDTH
C_QKV = 0
C_Q = C_QKV + GDN_QKV
C_Z = C_Q + NSA_WIDTH
C_KV = C_Z + GDN_WIDTH
SM_G3 = 0
SM_A = 3 * NSA_HEADS
SM_B = SM_A + GDN_HEADS
MOE_ROWS = 256
TOK_PAD = 256


def _cp(n_axes, vmem=V7X_VMEM_LIMIT_BYTES):
    return pltpu.CompilerParams(dimension_semantics=("arbitrary",) * n_axes, vmem_limit_bytes=vmem)


def _dot(a, b, precision=None):
    return jnp.dot(a, b, preferred_element_type=F32, precision=precision)


def _dot_nt(a, b, precision=None):
    return lax.dot_general(a, b, (((1,), (1,)), ((), ())), preferred_element_type=F32, precision=precision)


def _dot_tn(a, b, precision=None):
    return lax.dot_general(a, b, (((0,), (0,)), ((), ())), preferred_element_type=F32, precision=precision)


def _iota(shape, dim):
    return lax.broadcasted_iota(jnp.int32, shape, dim)


def _split3(x):
    hi = x.astype(BF16)
    r1 = x - hi.astype(F32)
    mid = r1.astype(BF16)
    lo = (r1 - mid.astype(F32)).astype(BF16)
    return hi, mid, lo


def _dot_f32_by_exact(x, m_bf16):
    hi, mid, lo = _split3(x)
    return _dot(hi, m_bf16) + _dot(mid, m_bf16) + _dot(lo, m_bf16)


def _masked_softmax(s, mask):
    s = jnp.where(mask, s, NEG_BIG)
    m = jnp.max(s, axis=-1, keepdims=True)
    e = jnp.where(mask, jnp.exp(s - m), 0.0)
    return e * (1.0 / jnp.maximum(jnp.sum(e, axis=-1, keepdims=True), 1e-30))


def _gelu_tanh(x):
    return x * (0.5 * (1.0 + jnp.tanh(np.sqrt(2.0 / np.pi).astype(np.float32) * (x + 0.044715 * (x * x * x)))))


def _topk_mask(s, k, axis):
    n = s.shape[axis]
    idx = _iota(s.shape, axis).astype(F32)
    sel = jnp.zeros(s.shape, F32)
    for _ in range(k):
        m = jnp.max(s, axis=axis, keepdims=True)
        first = jnp.min(jnp.where(s == m, idx, float(n)), axis=axis, keepdims=True)
        pick = idx == first
        sel = jnp.where(pick & (m > -jnp.inf), 1.0, sel)
        s = jnp.where(pick, -jnp.inf, s)
    return sel


def _rms_matmul_kernel(x_ref, g_ref, w_ref, o_ref, hn_ref, *, rc):
    tm = x_ref.shape[0]

    @pl.when(pl.program_id(1) == 0)
    def _():
        def body(c, carry):
            r = pl.multiple_of(c * rc, rc)
            x = x_ref[pl.ds(r, rc), :]
            y = x * lax.rsqrt(jnp.mean(x * x, axis=-1, keepdims=True) + RMS_EPS)
            hn_ref[pl.ds(r, rc), :] = (y * g_ref[...]).astype(hn_ref.dtype)
            return carry
        lax.fori_loop(0, tm // rc, body, 0)

    def mm(c, carry):
        r = pl.multiple_of(c * rc, rc)
        o_ref[pl.ds(r, rc), :] = _dot(hn_ref[pl.ds(r, rc), :], w_ref[...]).astype(o_ref.dtype)
        return carry
    lax.fori_loop(0, tm // rc, mm, 0)


def _row_tile(m):
    for t in (1280, 1024, 512, 256):
        if m % t == 0:
            return t
    raise ValueError(f"row count {m} must be a multiple of 256")


def _rms_matmul(x, gain, w, tn, out_dtype, name):
    m, d = x.shape
    n = w.shape[1]
    tm = _row_tile(m)
    rc = 256
    return pl.pallas_call(
        functools.partial(_rms_matmul_kernel, rc=rc),
        grid=(m // tm, n // tn),
        in_specs=[pl.BlockSpec((tm, d), lambda i, j: (i, 0)),
                  pl.BlockSpec((1, d), lambda i, j: (0, 0)),
                  pl.BlockSpec((d, tn), lambda i, j: (0, j))],
        out_specs=pl.BlockSpec((tm, tn), lambda i, j: (i, j)),
        out_shape=jax.ShapeDtypeStruct((m, n), out_dtype),
        scratch_shapes=[pltpu.VMEM((tm, d), BF16)],
        compiler_params=_cp(2),
        name=name,
    )(x, gain.reshape(1, d), w)


def _matmul_res_kernel(a_ref, w_ref, r_ref, o_ref, *, rc):
    tm = a_ref.shape[0]

    def mm(c, carry):
        r = pl.multiple_of(c * rc, rc)
        o_ref[pl.ds(r, rc), :] = r_ref[pl.ds(r, rc), :] + _dot(a_ref[pl.ds(r, rc), :], w_ref[...])
        return carry
    lax.fori_loop(0, tm // rc, mm, 0)


def _matmul_res(a, w, res, tn, name):
    m, k = a.shape
    n = w.shape[1]
    tm = _row_tile(m)
    return pl.pallas_call(
        functools.partial(_matmul_res_kernel, rc=256),
        grid=(m // tm, n // tn),
        in_specs=[pl.BlockSpec((tm, k), lambda i, j: (i, 0)),
                  pl.BlockSpec((k, tn), lambda i, j: (0, j)),
                  pl.BlockSpec((tm, tn), lambda i, j: (i, j))],
        out_specs=pl.BlockSpec((tm, tn), lambda i, j: (i, j)),
        out_shape=jax.ShapeDtypeStruct((m, n), F32),
        compiler_params=_cp(2),
        name=name,
    )(a, w, res)


def _upmix_kernel(on_ref, og_ref, wn_ref, wg_ref, ga_ref, gb_ref, o_ref, *, rc):
    tm = on_ref.shape[0]

    def mm(c, carry):
        r = pl.multiple_of(c * rc, rc)
        a = _dot(on_ref[pl.ds(r, rc), :], wn_ref[...])
        b = _dot(og_ref[pl.ds(r, rc), :], wg_ref[...])
        ga = jax.nn.sigmoid(ga_ref[pl.ds(r, rc), :])
        gb = jax.nn.sigmoid(gb_ref[pl.ds(r, rc), :])
        o_ref[pl.ds(r, rc), :] = (ga * a + gb * b).astype(o_ref.dtype)
        return carry
    lax.fori_loop(0, tm // rc, mm, 0)


def _upmix(o_nsa, o_gdn, w_n, w_g, proj, c_bg, d_model):
    m = o_nsa.shape[0]
    tm = _row_tile(m)
    tn = 512
    ca, cb = c_bg // tn, (c_bg + d_model) // tn
    return pl.pallas_call(
        functools.partial(_upmix_kernel, rc=256),
        grid=(m // tm, d_model // tn),
        in_specs=[pl.BlockSpec((tm, o_nsa.shape[1]), lambda i, j: (i, 0)),
                  pl.BlockSpec((tm, o_gdn.shape[1]), lambda i, j: (i, 0)),
                  pl.BlockSpec((w_n.shape[0], tn), lambda i, j: (0, j)),
                  pl.BlockSpec((w_g.shape[0], tn), lambda i, j: (0, j)),
                  pl.BlockSpec((tm, tn), lambda i, j: (i, ca + j)),
                  pl.BlockSpec((tm, tn), lambda i, j: (i, cb + j))],
        out_specs=pl.BlockSpec((tm, tn), lambda i, j: (i, j)),
        out_shape=jax.ShapeDtypeStruct((m, d_model), BF16),
        compiler_params=_cp(2),
        name="upmix",
    )(o_nsa, o_gdn, w_n, w_g, proj, proj)


def _cmp_kernel(pt_ref, pages_hbm, tail_hbm, w1_ref, pe_ref, b1_ref, w2_ref, out_ref, buf0, buf1, buf2, buf3, acc_ref,
                sem, *, pg_pages, n_pg):
    b = pl.program_id(0)
    pg = pl.program_id(1)
    m = SUBLANES * pg_pages
    rows = pg_pages * PAGE_SIZE
    bufs = (buf0, buf1, buf2, buf3)

    def page_copy(p, c):
        return pltpu.make_async_copy(pages_hbm.at[pt_ref[b, pg * pg_pages + p], :, pl.ds(c * HEAD_DIM, HEAD_DIM)],
                                     bufs[c].at[pl.ds(p * PAGE_SIZE, PAGE_SIZE)], sem.at[0])

    def start(p, carry):
        for c in range(4):
            page_copy(p, c).start()
        return carry
    lax.fori_loop(0, pg_pages, start, 0)

    @pl.when(pg == n_pg - 1)
    def _():
        cps = [pltpu.make_async_copy(tail_hbm.at[b, :, pl.ds(c * HEAD_DIM, HEAD_DIM)],
                                     bufs[c].at[pl.ds(rows, CMP_STRIDE)], sem.at[1]) for c in range(4)]
        for cp in cps:
            cp.start()
        for cp in cps:
            cp.wait()

    @pl.when(pg < n_pg - 1)
    def _():
        nxt = pt_ref[b, jnp.minimum((pg + 1) * pg_pages, n_pg * pg_pages - 1)]
        cps = [pltpu.make_async_copy(pages_hbm.at[nxt, pl.ds(0, CMP_STRIDE), pl.ds(c * HEAD_DIM, HEAD_DIM)],
                                     bufs[c].at[pl.ds(rows, CMP_STRIDE)], sem.at[1]) for c in range(4)]
        for cp in cps:
            cp.start()
        for cp in cps:
            cp.wait()

    def wait(p, carry):
        for c in range(4):
            page_copy(p, c).wait()
        return carry
    lax.fori_loop(0, pg_pages, wait, 0)

    for kv in range(2):
        for g in range(NSA_KV_HEADS):
            buf = bufs[g * 2 + kv]
            for s in range(CMP_STRIDE):
                a0 = buf[pl.ds(s, m, stride=CMP_STRIDE), :] + pe_ref[kv, pl.ds(s, 1), pl.ds(0, HEAD_DIM)]
                a1 = (buf[pl.ds(CMP_STRIDE + s, m, stride=CMP_STRIDE), :]
                      + pe_ref[kv, pl.ds(s, 1), pl.ds(HEAD_DIM, HEAD_DIM)])
                part = _dot(jnp.concatenate([a0, a1], axis=1).astype(BF16), w1_ref[kv, s])
                if s == 0:
                    acc_ref[...] = part
                else:
                    acc_ref[...] += part
            h = _gelu_tanh(acc_ref[...] + b1_ref[pl.ds(kv, 1), :])
            out_ref[kv, g] = _dot(h.astype(BF16), w2_ref[kv]).astype(out_ref.dtype)


def _compress(pages, page_table, tail, cmp_pe, cmp_w1, cmp_b1, cmp_w2):
    nseq, n_pages = page_table.shape
    pg_pages = min(64, n_pages)
    assert n_pages % pg_pages == 0
    n_pg = n_pages // pg_pages
    m = SUBLANES * pg_pages
    r_blk = CMP_BLOCK // CMP_STRIDE
    w1 = cmp_w1.reshape(2, r_blk, CMP_STRIDE, HEAD_DIM, CMP_HIDDEN).transpose(0, 2, 1, 3, 4)
    w1 = w1.reshape(2, CMP_STRIDE, r_blk * HEAD_DIM, CMP_HIDDEN).astype(BF16)
    pe = cmp_pe.reshape(2, r_blk, CMP_STRIDE, HEAD_DIM).transpose(0, 2, 1, 3).reshape(2, CMP_STRIDE, r_blk * HEAD_DIM)
    grid_spec = pltpu.PrefetchScalarGridSpec(
        num_scalar_prefetch=1,
        grid=(nseq, n_pg),
        in_specs=[pl.BlockSpec(memory_space=pl.ANY),
                  pl.BlockSpec(memory_space=pl.ANY),
                  pl.BlockSpec(w1.shape, lambda b, p, pt: (0, 0, 0, 0)),
                  pl.BlockSpec(pe.shape, lambda b, p, pt: (0, 0, 0)),
                  pl.BlockSpec(cmp_b1.shape, lambda b, p, pt: (0, 0)),
                  pl.BlockSpec(cmp_w2.shape, lambda b, p, pt: (0, 0, 0))],
        out_specs=pl.BlockSpec((None, 2, NSA_KV_HEADS, m, HEAD_DIM), lambda b, p, pt: (b, 0, 0, p, 0)),
        scratch_shapes=[pltpu.VMEM((pg_pages * PAGE_SIZE + CMP_STRIDE, HEAD_DIM), F32)] * 4 + [
                        pltpu.VMEM((m, CMP_HIDDEN), F32),
                        pltpu.SemaphoreType.DMA((2,))],
    )
    return pl.pallas_call(
        functools.partial(_cmp_kernel, pg_pages=pg_pages, n_pg=n_pg),
        grid_spec=grid_spec,
        out_shape=jax.ShapeDtypeStruct((nseq, 2, NSA_KV_HEADS, n_pages * SUBLANES, HEAD_DIM), BF16),
        compiler_params=_cp(2),
        name="nsa_compress",
    )(page_table, pages, tail, w1, pe, cmp_b1, cmp_w2.astype(BF16))


def _selection_matrix(n_cmp, n_sel, n_sel_pad):
    per_sel = SLC_BLOCK // CMP_STRIDE
    msel = np.zeros((n_cmp, n_sel_pad), np.float32)
    for j in range(n_sel):
        for c, wgt in ((per_sel * j - 1, 1.0), (per_sel * j, 2.0), (per_sel * j + 1, 2.0), (per_sel * j + 2, 2.0),
                       (per_sel * j + 3, 1.0)):
            if 0 <= c < n_cmp:
                msel[c, j] = wgt
    return jnp.asarray(msel, BF16)


def _nsa_prompt_kernel(q_ref, sm_ref, kcvc_ref, msel_ref, kslc_ref, vslc_ref, *rest, tk):
    kw_refs = rest[0:5]
    vw_refs = rest[5:10]
    o_ref, selt_ref, m_ref, l_ref, acc_ref = rest[10:]
    g = pl.program_id(0)
    i = pl.program_id(1)
    nq = Q_BLOCK
    scale = HEAD_DIM ** -0.5
    q_pos = i * nq + _iota((nq, 1), 0)
    q4 = q_ref[...] * scale
    qs = jnp.concatenate([q4[:, r * HEAD_DIM:(r + 1) * HEAD_DIM] for r in range(NSA_REP)], axis=0).astype(BF16)

    kc = kcvc_ref[0]
    vc = kcvc_ref[1]
    n_cmp = kc.shape[0]
    s_c = _dot_nt(qs, kc)
    c_end = _iota((1, n_cmp), 1) * CMP_STRIDE + (CMP_BLOCK - 1)
    cmask = c_end <= q_pos
    p_c = [_masked_softmax(s_c[r * nq:(r + 1) * nq], cmask) for r in range(NSA_REP)]
    o_c = _dot(jnp.concatenate(p_c, axis=0).astype(BF16), vc)

    psum = p_c[0]
    for r in range(1, NSA_REP):
        psum = psum + p_c[r]
    imp_t = _dot_f32_by_exact(psum, msel_ref[...]).T
    jblk = _iota(imp_t.shape, 0)
    q_row = i * nq + _iota((1, nq), 1)
    cur = q_row // SLC_BLOCK
    forced = (jblk == 0) | (jblk == cur) | (jblk == cur - 1)
    causal = jblk * SLC_BLOCK <= q_row
    s_sel = jnp.where(forced, jnp.inf, imp_t)
    s_sel = jnp.where(causal, s_sel, -jnp.inf)
    selt_ref[...] = _topk_mask(s_sel, N_SLC, axis=0)

    m_ref[...] = jnp.full(m_ref.shape, NEG_BIG, F32)
    l_ref[...] = jnp.zeros(l_ref.shape, F32)
    acc_ref[...] = jnp.zeros(acc_ref.shape, F32)
    blk_per_tile = tk // SLC_BLOCK
    n_kt = (i * nq + nq + tk - 1) // tk

    def tile(kt, carry):
        k0 = pl.multiple_of(kt * tk, tk)
        kt_k = kslc_ref[pl.ds(k0, tk), :]
        kt_v = vslc_ref[pl.ds(k0, tk), :]
        s = _dot_nt(qs, kt_k)
        rows = selt_ref[pl.ds(pl.multiple_of(kt * blk_per_tile, blk_per_tile), blk_per_tile), :]
        exp_t = jnp.concatenate([jnp.broadcast_to(rows[j:j + 1, :], (SLC_BLOCK, nq)) for j in range(blk_per_tile)],
                                axis=0)
        kpos = k0 + _iota((1, tk), 1)
        valid = (exp_t.T > 0.5) & (kpos <= q_pos)
        ps = []
        for r in range(NSA_REP):
            sl = slice(r * nq, (r + 1) * nq)
            sr = jnp.where(valid, s[sl], NEG_BIG)
            m_old = m_ref[sl]
            m_new = jnp.maximum(m_old, jnp.max(sr, axis=-1, keepdims=True))
            alpha = jnp.exp(m_old - m_new)
            p = jnp.where(valid, jnp.exp(sr - m_new), 0.0)
            l_ref[sl] = alpha * l_ref[sl] + jnp.sum(p, axis=-1, keepdims=True)
            acc_ref[sl] = alpha * acc_ref[sl]
            m_ref[sl] = m_new
            ps.append(p)
        acc_ref[...] += _dot(jnp.concatenate(ps, axis=0).astype(BF16), kt_v)
        return carry
    lax.fori_loop(0, n_kt, tile, 0)
    o_s = acc_ref[...] * (1.0 / jnp.maximum(l_ref[...], 1e-30))

    kw = jnp.concatenate([r_[...] for r_ in kw_refs], axis=0)
    vw = jnp.concatenate([r_[...] for r_ in vw_refs], axis=0)
    n_w = kw.shape[0]
    s_w = _dot_nt(qs, kw)
    w_pos = i * nq - WINDOW + _iota((1, n_w), 1)
    dist = q_pos - w_pos
    wmask = (dist >= 0) & (dist < WINDOW) & (w_pos >= 0)
    p_w = [_masked_softmax(s_w[r * nq:(r + 1) * nq], wmask) for r in range(NSA_REP)]
    o_w = _dot(jnp.concatenate(p_w, axis=0).astype(BF16), vw)

    gates = jax.nn.sigmoid(sm_ref[...])
    lane = _iota(gates.shape, 1)
    outs = []
    for r in range(NSA_REP):
        sl = slice(r * nq, (r + 1) * nq)
        base = SM_G3 + g * (NSA_REP * 3) + r * 3
        gc = jnp.sum(jnp.where(lane == base, gates, 0.0), axis=-1, keepdims=True)
        gs = jnp.sum(jnp.where(lane == base + 1, gates, 0.0), axis=-1, keepdims=True)
        gw = jnp.sum(jnp.where(lane == base + 2, gates, 0.0), axis=-1, keepdims=True)
        outs.append(gc * o_c[sl] + gs * o_s[sl] + gw * o_w[sl])
    o_ref[...] = jnp.concatenate(outs, axis=1).astype(o_ref.dtype)


def _nsa_prompt(proj, kcvc, s_len, c_sm):
    n_cmp = kcvc.shape[3]
    n_sel = -(-s_len // SLC_BLOCK)
    n_sel_pad = -(-n_sel // LANES) * LANES
    tk = 512
    assert s_len % tk == 0 and s_len % Q_BLOCK == 0
    msel = _selection_matrix(n_cmp, n_sel, n_sel_pad)
    slc = proj[:s_len, C_KV + KV_BRANCH:C_KV + 2 * KV_BRANCH].astype(BF16)
    win = proj[:s_len, C_KV + 2 * KV_BRANCH:C_KV + 3 * KV_BRANCH].astype(BF16)
    win = jnp.pad(win, ((WINDOW, 0), (0, 0)))
    n_wblk = WINDOW // Q_BLOCK + 1
    qcol = C_Q // (NSA_REP * HEAD_DIM)
    smcol = c_sm // LANES
    in_specs = [pl.BlockSpec((Q_BLOCK, NSA_REP * HEAD_DIM), lambda g, i: (i, qcol + g)),
                pl.BlockSpec((Q_BLOCK, LANES), lambda g, i: (i, smcol)),
                pl.BlockSpec((None, 2, None, n_cmp, HEAD_DIM), lambda g, i: (0, 0, g, 0, 0)),
                pl.BlockSpec(msel.shape, lambda g, i: (0, 0)),
                pl.BlockSpec((s_len, HEAD_DIM), lambda g, i: (0, 2 * g)),
                pl.BlockSpec((s_len, HEAD_DIM), lambda g, i: (0, 2 * g + 1))]
    in_specs += [pl.BlockSpec((Q_BLOCK, HEAD_DIM), functools.partial(lambda g, i, j: (i + j, 2 * g), j=j))
                 for j in range(n_wblk)]
    in_specs += [pl.BlockSpec((Q_BLOCK, HEAD_DIM), functools.partial(lambda g, i, j: (i + j, 2 * g + 1), j=j))
                 for j in range(n_wblk)]
    rows = NSA_REP * Q_BLOCK
    return pl.pallas_call(
        functools.partial(_nsa_prompt_kernel, tk=tk),
        grid=(NSA_KV_HEADS, s_len // Q_BLOCK),
        in_specs=in_specs,
        out_specs=pl.BlockSpec((Q_BLOCK, NSA_REP * HEAD_DIM), lambda g, i: (i, g)),
        out_shape=jax.ShapeDtypeStruct((s_len, NSA_WIDTH), BF16),
        scratch_shapes=[pltpu.VMEM((n_sel_pad, Q_BLOCK), F32),
                        pltpu.VMEM((rows, 1), F32),
                        pltpu.VMEM((rows, 1), F32),
                        pltpu.VMEM((rows, HEAD_DIM), F32)],
        compiler_params=_cp(2),
        name="nsa_prompt",
    )(proj, proj, kcvc, msel, slc, slc, *([win] * (2 * n_wblk)))


def _nsa_sample_kernel(pt_ref, q_ref, sm_ref, nslc_ref, nwin_ref, kcvc_ref, msel_ref, eexp_ref, pages_hbm, wbuf_ref,
                       o_ref, buf, sel_ref, m_ref, l_ref, acc_ref, oc_ref, sem, *, pg_pages, n_pg, ts, past):
    b = pl.program_id(0)
    pg = pl.program_id(1)
    n_rows = NSA_REP * ts
    keys = pg_pages * PAGE_SIZE
    scale = HEAD_DIM ** -0.5

    def page_copy(p):
        return pltpu.make_async_copy(pages_hbm.at[pt_ref[b, pg * pg_pages + p]],
                                     buf.at[pl.ds(p * PAGE_SIZE, PAGE_SIZE)], sem.at[0])

    def start(p, carry):
        page_copy(p).start()
        return carry
    lax.fori_loop(0, pg_pages, start, 0)

    t_of_row = _iota((n_rows, 1), 0) & (ts - 1)
    q_pos = past + t_of_row

    def q_group(g):
        qg = q_ref[:, pl.ds(g * NSA_REP * HEAD_DIM, NSA_REP * HEAD_DIM)] * scale
        return jnp.concatenate([qg[:, r * HEAD_DIM:(r + 1) * HEAD_DIM] for r in range(NSA_REP)], axis=0).astype(BF16)

    @pl.when(pg == 0)
    def _():
        for g in range(NSA_KV_HEADS):
            qs = q_group(g)
            kc = kcvc_ref[0, g]
            vc = kcvc_ref[1, g]
            n_cmp = kc.shape[0]
            s_c = _dot_nt(qs, kc)
            c_end = _iota((1, n_cmp), 1) * CMP_STRIDE + (CMP_BLOCK - 1)
            p_c = _masked_softmax(s_c, c_end <= q_pos)
            oc_ref[g] = _dot(p_c.astype(BF16), vc)
            psum = p_c[0:ts]
            for r in range(1, NSA_REP):
                psum = psum + p_c[r * ts:(r + 1) * ts]
            imp = _dot_f32_by_exact(psum, msel_ref[...])
            jblk = _iota(imp.shape, 1)
            qp = past + _iota((ts, 1), 0)
            cur = qp // SLC_BLOCK
            forced = (jblk == 0) | (jblk == cur) | (jblk == cur - 1)
            causal = jblk * SLC_BLOCK <= qp
            s_sel = jnp.where(forced, jnp.inf, imp)
            s_sel = jnp.where(causal, s_sel, -jnp.inf)
            sel = _topk_mask(s_sel, N_SLC, axis=1)
            sel_ref[g] = jnp.concatenate([sel] * NSA_REP, axis=0)
            m_ref[g] = jnp.full((n_rows, 1), NEG_BIG, F32)
            l_ref[g] = jnp.zeros((n_rows, 1), F32)
            acc_ref[g] = jnp.zeros((n_rows, HEAD_DIM), F32)

    def wait(p, carry):
        page_copy(p).wait()
        return carry
    lax.fori_loop(0, pg_pages, wait, 0)

    def online_update(g, s, valid, v):
        sr = jnp.where(valid, s, NEG_BIG)
        m_old = m_ref[g]
        m_new = jnp.maximum(m_old, jnp.max(sr, axis=-1, keepdims=True))
        alpha = jnp.exp(m_old - m_new)
        p = jnp.where(valid, jnp.exp(sr - m_new), 0.0)
        l_ref[g] = alpha * l_ref[g] + jnp.sum(p, axis=-1, keepdims=True)
        acc_ref[g] = alpha * acc_ref[g] + _dot(p.astype(BF16), v)
        m_ref[g] = m_new

    n_sel_pad = sel_ref.shape[2]
    blk_in_grp = pg_pages * (PAGE_SIZE // SLC_BLOCK)
    onehot = (_iota((n_sel_pad, LANES), 0) == pg * blk_in_grp + _iota((n_sel_pad, LANES), 1)) & (
        _iota((n_sel_pad, LANES), 1) < blk_in_grp)
    onehot = jnp.where(onehot, 1.0, 0.0).astype(BF16)
    kpos = pg * keys + _iota((1, keys), 1)
    for g in range(NSA_KV_HEADS):
        qs = q_group(g)
        kk = buf[:, pl.ds(g * 2 * HEAD_DIM, HEAD_DIM)].astype(BF16)
        vv = buf[:, pl.ds(g * 2 * HEAD_DIM + HEAD_DIM, HEAD_DIM)].astype(BF16)
        s = _dot_nt(qs, kk)
        sel_grp = _dot(sel_ref[g].astype(BF16), onehot)
        maskf = _dot(sel_grp.astype(BF16), eexp_ref[...])
        online_update(g, s, (maskf > 0.5) & (kpos <= q_pos), vv)

    @pl.when(pg == n_pg - 1)
    def _():
        zpad = jnp.zeros((LANES - ts, HEAD_DIM), F32)
        lane = _iota((1, LANES), 1)
        npos = past + lane
        new_blk = past // SLC_BLOCK
        gates = jax.nn.sigmoid(sm_ref[...])
        glane = _iota(gates.shape, 1)
        wb_rows = wbuf_ref.shape[0]
        outs = []
        for g in range(NSA_KV_HEADS):
            qs = q_group(g)
            kn = jnp.concatenate([nslc_ref[:, pl.ds(g * 2 * HEAD_DIM, HEAD_DIM)], zpad], axis=0).astype(BF16)
            vn = jnp.concatenate([nslc_ref[:, pl.ds(g * 2 * HEAD_DIM + HEAD_DIM, HEAD_DIM)], zpad], axis=0).astype(BF16)
            s = _dot_nt(qs, kn)
            selcol = sel_ref[g][:, new_blk:new_blk + 1]
            online_update(g, s, (selcol > 0.5) & (lane < ts) & (npos <= q_pos), vn)
            o_s = acc_ref[g] * (1.0 / jnp.maximum(l_ref[g], 1e-30))

            kwn = jnp.concatenate([nwin_ref[:, pl.ds(g * 2 * HEAD_DIM, HEAD_DIM)], zpad], axis=0)
            vwn = jnp.concatenate([nwin_ref[:, pl.ds(g * 2 * HEAD_DIM + HEAD_DIM, HEAD_DIM)], zpad], axis=0)
            kw = jnp.concatenate([wbuf_ref[:, pl.ds(g * 2 * HEAD_DIM, HEAD_DIM)], kwn], axis=0).astype(BF16)
            vw = jnp.concatenate([wbuf_ref[:, pl.ds(g * 2 * HEAD_DIM + HEAD_DIM, HEAD_DIM)], vwn], axis=0).astype(BF16)
            n_w = kw.shape[0]
            widx = _iota((1, n_w), 1)
            w_pos = past - wb_rows + widx
            dist = q_pos - w_pos
            wmask = (dist >= 0) & (dist < WINDOW) & (w_pos >= 0) & (widx < wb_rows + ts)
            p_w = _masked_softmax(_dot_nt(qs, kw), wmask)
            o_w = _dot(p_w.astype(BF16), vw)
            o_c = oc_ref[g]
            for r in range(NSA_REP):
                sl = slice(r * ts, (r + 1) * ts)
                base = SM_G3 + g * (NSA_REP * 3) + r * 3
                gc = jnp.sum(jnp.where(glane == base, gates, 0.0), axis=-1, keepdims=True)
                gs = jnp.sum(jnp.where(glane == base + 1, gates, 0.0), axis=-1, keepdims=True)
                gw = jnp.sum(jnp.where(glane == base + 2, gates, 0.0), axis=-1, keepdims=True)
                outs.append(gc * o_c[sl] + gs * o_s[sl] + gw * o_w[sl])
        o_ref[...] = jnp.concatenate(outs, axis=1)


def _nsa_sample(proj, kcvc, cache_slc, cache_win, page_table, s_len, ts, c_sm):
    nb, n_pages = page_table.shape
    past = n_pages * PAGE_SIZE
    assert ts == SUBLANES and (ts & (ts - 1)) == 0 and s_len % ts == 0
    n_cmp = kcvc.shape[3]
    n_sel = -(-(past + ts) // SLC_BLOCK)
    n_sel_pad = -(-n_sel // LANES) * LANES
    pg_pages = min(64, n_pages)
    assert n_pages % pg_pages == 0 and 2 * pg_pages <= LANES
    n_pg = n_pages // pg_pages
    keys = pg_pages * PAGE_SIZE
    msel = _selection_matrix(n_cmp, n_sel, n_sel_pad)
    eexp = np.zeros((LANES, keys), np.float32)
    for k in range(keys):
        eexp[k // SLC_BLOCK, k] = 1.0
    eexp = jnp.asarray(eexp, BF16)
    pages = cache_slc.reshape(cache_slc.shape[0], PAGE_SIZE, KV_BRANCH)
    wbuf = cache_win.reshape(nb, cache_win.shape[1], KV_BRANCH)
    row0 = s_len // ts
    n_rows = NSA_REP * ts
    grid_spec = pltpu.PrefetchScalarGridSpec(
        num_scalar_prefetch=1,
        grid=(nb, n_pg),
        in_specs=[pl.BlockSpec((ts, NSA_WIDTH), lambda b, p, pt: (row0 + b, C_Q // NSA_WIDTH)),
                  pl.BlockSpec((ts, LANES), lambda b, p, pt: (row0 + b, c_sm // LANES)),
                  pl.BlockSpec((ts, KV_BRANCH), lambda b, p, pt: (row0 + b, C_KV // KV_BRANCH + 1)),
                  pl.BlockSpec((ts, KV_BRANCH), lambda b, p, pt: (row0 + b, C_KV // KV_BRANCH + 2)),
                  pl.BlockSpec((None, 2, NSA_KV_HEADS, n_cmp, HEAD_DIM), lambda b, p, pt: (b, 0, 0, 0, 0)),
                  pl.BlockSpec(msel.shape, lambda b, p, pt: (0, 0)),
                  pl.BlockSpec(eexp.shape, lambda b, p, pt: (0, 0)),
                  pl.BlockSpec(memory_space=pl.ANY),
                  pl.BlockSpec((None, wbuf.shape[1], KV_BRANCH), lambda b, p, pt: (b, 0, 0))],
        out_specs=pl.BlockSpec((ts, NSA_WIDTH), lambda b, p, pt: (b, 0)),
        scratch_shapes=[pltpu.VMEM((keys, KV_BRANCH), F32),
                        pltpu.VMEM((NSA_KV_HEADS, n_rows, n_sel_pad), F32),
                        pltpu.VMEM((NSA_KV_HEADS, n_rows, 1), F32),
                        pltpu.VMEM((NSA_KV_HEADS, n_rows, 1), F32),
                        pltpu.VMEM((NSA_KV_HEADS, n_rows, HEAD_DIM), F32),
                        pltpu.VMEM((NSA_KV_HEADS, n_rows, HEAD_DIM), F32),
                        pltpu.SemaphoreType.DMA((1,))],
    )
    return pl.pallas_call(
        functools.partial(_nsa_sample_kernel, pg_pages=pg_pages, n_pg=n_pg, ts=ts, past=past),
        grid_spec=grid_spec,
        out_shape=jax.ShapeDtypeStruct((nb * ts, NSA_WIDTH), F32),
        compiler_params=_cp(2),
        name="nsa_sample",
    )(page_table, proj, proj, proj, proj, kcvc, msel, eexp, pages, wbuf)


def _lane_pick(x, lane_idx):
    return jnp.sum(jnp.where(_iota(x.shape, 1) == lane_idx, x, 0.0), axis=-1, keepdims=True)


def _causal_conv_silu(xp_ref, w_ref, n):
    y = xp_ref[pl.ds(5, n), :] * w_ref[pl.ds(0, 1), :]
    for t in range(1, CONV_W):
        y = y + xp_ref[pl.ds(5 + t, n), :] * w_ref[pl.ds(t, 1), :]
    return y * jax.nn.sigmoid(y)


def _l2norm(x):
    return x * lax.rsqrt(jnp.sum(x * x, axis=-1, keepdims=True) + 1e-6)


def _gdn_chunk_prepare(q, k, v, g, beta):
    c = q.shape[0]
    ii = _iota((c, c), 0)
    jj = _iota((c, c), 1)
    ltri = jnp.where(jj <= ii, 1.0, 0.0)
    gb = _dot(ltri, jnp.broadcast_to(g, (c, LANES)), HI)
    g_col = gb[:, 0:c]
    g_row = _dot(jnp.ones((c, c), F32), jnp.where(ii == jj, g_col, 0.0), HI)
    decay = jnp.where(ii >= jj, jnp.exp(g_col - g_row), 0.0)
    kb = k * beta
    a = jnp.where(ii > jj, _dot_nt(kb, k, HI) * decay, 0.0)
    eye = jnp.where(ii == jj, 1.0, 0.0)
    pw = -a
    t_inv = eye + pw
    n_sq = int(np.ceil(np.log2(c))) - 1
    for _ in range(n_sq):
        pw = _dot(pw, pw, HI)
        t_inv = t_inv + _dot(t_inv, pw, HI)
    eg = jnp.exp(gb)
    w = _dot(t_inv, kb * eg, HI)
    u = _dot(t_inv, v * beta, HI)
    qk = _dot_nt(q, k, HI) * decay
    g_last = gb[c - 1:c, :]
    return dict(w=w, u=u, qk=qk, qe=q * eg, kd=k * jnp.exp(g_last - gb), eg_last=jnp.exp(g_last))


def _gdn_chunk_apply(pre, state):
    v_new = pre["u"] - _dot(pre["w"], state, HI)
    o = _dot(pre["qe"], state, HI) + _dot(pre["qk"], v_new, HI)
    state = state * pre["eg_last"] + _dot_tn(pre["kd"], v_new, HI)
    return o, state


def _gdn_gates(sm, h, alog_ref, dtb_ref):
    a = _lane_pick(sm, SM_A + h)
    bq = _lane_pick(sm, SM_B + h)
    alog = _lane_pick(alog_ref[...], h)
    dtb = _lane_pick(dtb_ref[...], h)
    x = a + dtb
    softplus = jnp.maximum(x, 0.0) + jnp.log(1.0 + jnp.exp(-jnp.abs(x)))
    return -jnp.exp(alog) * softplus, jax.nn.sigmoid(bq)


def _gdn_out(o, z, ng_ref):
    y = o * lax.rsqrt(jnp.mean(o * o, axis=-1, keepdims=True) + RMS_EPS) * ng_ref[...]
    return y * (z * jax.nn.sigmoid(z))


def _gdn_prompt_kernel(xq_ref, xk_ref, xv_ref, z_ref, sm_ref, wq_ref, wk_ref, wv_ref, alog_ref, dtb_ref, ng_ref,
                       o_ref, st_ref, xpq, xpk, xpv, s_ref, *, tb):
    h = pl.program_id(0)
    i = pl.program_id(1)

    @pl.when(i == 0)
    def _():
        for xp in (xpq, xpk, xpv):
            xp[pl.ds(0, SUBLANES), :] = jnp.zeros((SUBLANES, LANES), F32)
        s_ref[...] = jnp.zeros(s_ref.shape, F32)

    conv = []
    for xp, x_ref, w_ref in ((xpq, xq_ref, wq_ref), (xpk, xk_ref, wk_ref), (xpv, xv_ref, wv_ref)):
        xp[pl.ds(SUBLANES, tb), :] = x_ref[...]
        conv.append(_causal_conv_silu(xp, w_ref, tb))
        xp[pl.ds(SUBLANES - (CONV_W - 1), CONV_W - 1), :] = xp[pl.ds(SUBLANES + tb - (CONV_W - 1), CONV_W - 1), :]
    q = _l2norm(conv[0]) * (GDN_DK ** -0.5)
    k = _l2norm(conv[1])
    v = conv[2]
    g, beta = _gdn_gates(sm_ref[...], h, alog_ref, dtb_ref)
    c = GDN_CHUNK
    pres = [_gdn_chunk_prepare(q[n * c:(n + 1) * c], k[n * c:(n + 1) * c], v[n * c:(n + 1) * c],
                               g[n * c:(n + 1) * c], beta[n * c:(n + 1) * c]) for n in range(tb // c)]
    state = s_ref[...]
    for n, pre in enumerate(pres):
        o, state = _gdn_chunk_apply(pre, state)
        o_ref[pl.ds(n * c, c), :] = _gdn_out(o, z_ref[pl.ds(n * c, c), :], ng_ref).astype(o_ref.dtype)
    s_ref[...] = state
    st_ref[...] = state


def _gdn_prompt(proj, s_len, conv_w, alog, dtb, norm_g, c_sm):
    tb = 256 if s_len % 256 == 0 else GDN_CHUNK
    assert s_len % tb == 0
    hb = GDN_DK // LANES
    cq, ck, cv, cz = C_QKV // LANES, C_QKV // LANES + GDN_HEADS * hb, C_QKV // LANES + 2 * GDN_HEADS * hb, C_Z // LANES
    row = lambda off: pl.BlockSpec((tb, LANES), lambda h, i: (i, off + h))
    wsp = lambda off: pl.BlockSpec((CONV_W, LANES), lambda h, i: (0, off + h))
    vec = pl.BlockSpec((1, LANES), lambda h, i: (0, 0))
    return pl.pallas_call(
        functools.partial(_gdn_prompt_kernel, tb=tb),
        grid=(GDN_HEADS, s_len // tb),
        in_specs=[row(cq), row(ck), row(cv), row(cz), pl.BlockSpec((tb, LANES), lambda h, i: (i, c_sm // LANES)),
                  wsp(0), wsp(GDN_HEADS), wsp(2 * GDN_HEADS), vec, vec, vec],
        out_specs=[pl.BlockSpec((tb, LANES), lambda h, i: (i, h)),
                   pl.BlockSpec((None, GDN_DK, GDN_DV), lambda h, i: (h, 0, 0))],
        out_shape=[jax.ShapeDtypeStruct((s_len, GDN_WIDTH), BF16),
                   jax.ShapeDtypeStruct((GDN_HEADS, GDN_DK, GDN_DV), F32)],
        scratch_shapes=[pltpu.VMEM((SUBLANES + tb, LANES), F32)] * 3 + [pltpu.VMEM((GDN_DK, GDN_DV), F32)],
        compiler_params=_cp(2),
        name="gdn_prompt",
    )(proj, proj, proj, proj, proj, conv_w, conv_w, conv_w, alog, dtb, norm_g)


def _gdn_sample_kernel(xq_ref, xk_ref, xv_ref, z_ref, sm_ref, cq_ref, ck_ref, cv_ref, s0_ref, wq_ref, wk_ref, wv_ref,
                       alog_ref, dtb_ref, ng_ref, o_ref, st_ref, xpq, xpk, xpv, *, ts):
    h = pl.program_id(0)
    conv = []
    for xp, x_ref, c_ref, w_ref in ((xpq, xq_ref, cq_ref, wq_ref), (xpk, xk_ref, ck_ref, wk_ref),
                                    (xpv, xv_ref, cv_ref, wv_ref)):
        xp[pl.ds(0, SUBLANES), :] = jnp.zeros((SUBLANES, LANES), F32)
        xp[pl.ds(SUBLANES - (CONV_W - 1), CONV_W - 1), :] = c_ref[...]
        xp[pl.ds(SUBLANES, ts), :] = x_ref[...]
        conv.append(_causal_conv_silu(xp, w_ref, ts))
    g, beta = _gdn_gates(sm_ref[...], h, alog_ref, dtb_ref)
    c = GDN_CHUNK
    pad = lambda x: jnp.concatenate([x, jnp.zeros((c - ts, x.shape[1]), F32)], axis=0)
    q = pad(_l2norm(conv[0]) * (GDN_DK ** -0.5))
    k = pad(_l2norm(conv[1]))
    v = pad(conv[2])
    pre = _gdn_chunk_prepare(q, k, v, pad(g), pad(beta))
    o, state = _gdn_chunk_apply(pre, s0_ref[...])
    o_ref[...] = _gdn_out(o[0:ts], z_ref[...], ng_ref)
    st_ref[...] = state


def _gdn_sample(proj, s_len, ts, state0, conv0, conv_w, alog, dtb, norm_g, c_sm):
    nb = state0.shape[0]
    row0 = s_len // ts
    hb = GDN_DK // LANES
    cq, ck, cv, cz = C_QKV // LANES, C_QKV // LANES + GDN_HEADS * hb, C_QKV // LANES + 2 * GDN_HEADS * hb, C_Z // LANES
    row = lambda off: pl.BlockSpec((ts, LANES), lambda h, b: (row0 + b, off + h))
    cst = lambda off: pl.BlockSpec((None, CONV_W - 1, LANES), lambda h, b: (b, 0, off + h))
    wsp = lambda off: pl.BlockSpec((CONV_W, LANES), lambda h, b: (0, off + h))
    vec = pl.BlockSpec((1, LANES), lambda h, b: (0, 0))
    st = pl.BlockSpec((None, None, GDN_DK, GDN_DV), lambda h, b: (b, h, 0, 0))
    return pl.pallas_call(
        functools.partial(_gdn_sample_kernel, ts=ts),
        grid=(GDN_HEADS, nb),
        in_specs=[row(cq), row(ck), row(cv), row(cz), pl.BlockSpec((ts, LANES), lambda h, b: (row0 + b, c_sm // LANES)),
                  cst(0), cst(GDN_HEADS), cst(2 * GDN_HEADS), st,
                  wsp(0), wsp(GDN_HEADS), wsp(2 * GDN_HEADS), vec, vec, vec],
        out_specs=[pl.BlockSpec((ts, LANES), lambda h, b: (b, h)), st],
        out_shape=[jax.ShapeDtypeStruct((nb * ts, GDN_WIDTH), F32),
                   jax.ShapeDtypeStruct(state0.shape, F32)],
        scratch_shapes=[pltpu.VMEM((SUBLANES + ts, LANES), F32)] * 3,
        compiler_params=_cp(2),
        name="gdn_sample",
    )(proj, proj, proj, proj, proj, conv0, conv0, conv0, state0, conv_w, conv_w, conv_w, alog, dtb, norm_g)


def _mem_attn_kernel(q_ref, kv_ref, o_ref):
    d = q_ref.shape[1]
    hd = d // MEM_HEADS
    outs = []
    for h in range(MEM_HEADS):
        qh = q_ref[:, pl.ds(h * hd, hd)].astype(BF16)
        kh = kv_ref[:, pl.ds(h * hd, hd)].astype(BF16)
        vh = kv_ref[:, pl.ds(d + h * hd, hd)].astype(BF16)
        s = _dot_nt(qh, kh) * (hd ** -0.5)
        m = jnp.max(s, axis=-1, keepdims=True)
        e = jnp.exp(s - m)
        p = e * (1.0 / jnp.sum(e, axis=-1, keepdims=True))
        outs.append(_dot(p.astype(BF16), vh))
    o_ref[...] = jnp.concatenate(outs, axis=1).astype(o_ref.dtype)


def _mem_attn(qm, kv, row0, rows, tb, per_seq, name):
    d = qm.shape[1]
    mt = kv.shape[1]
    n = rows // tb
    kv_map = (lambda i: (i, 0, 0)) if per_seq else (lambda i: (0, 0, 0))
    return pl.pallas_call(
        _mem_attn_kernel,
        grid=(n,),
        in_specs=[pl.BlockSpec((tb, d), lambda i: (row0 // tb + i, 0)),
                  pl.BlockSpec((None, mt, 2 * d), kv_map)],
        out_specs=pl.BlockSpec((tb, d), lambda i: (i, 0)),
        out_shape=jax.ShapeDtypeStruct((rows, d), F32),
        compiler_params=_cp(1),
        name=name,
    )(qm, kv)


R_E0, R_E1, R_W0, R_W1, R_RANK0, R_RANK1 = 0, 1, 2, 3, 4, 5
EXP_LANE0 = N_GROUPS


def _router_kernel(x_ref, g_ref, w_ref, bias_ref, h_ref, route_ref, cnt_ref, carry_ref, *, n_valid):
    i = pl.program_id(0)
    tm = x_ref.shape[0]

    @pl.when(i == 0)
    def _():
        carry_ref[...] = jnp.zeros(carry_ref.shape, F32)

    x = x_ref[...]
    hn = x * lax.rsqrt(jnp.mean(x * x, axis=-1, keepdims=True) + RMS_EPS) * g_ref[...]
    h_ref[...] = hn
    logits = _dot(hn, w_ref[...], HI) + bias_ref[...]
    lane = _iota(logits.shape, 1)
    lane_f = lane.astype(F32)

    def softmax_on(mask):
        z = jnp.where(mask, logits, NEG_BIG)
        e = jnp.where(mask, jnp.exp(z - jnp.max(z, axis=-1, keepdims=True)), 0.0)
        return e / jnp.sum(e, axis=-1, keepdims=True)

    def first_max(p, mask):
        v = jnp.max(jnp.where(mask, p, -1.0), axis=-1, keepdims=True)
        idx = jnp.min(jnp.where(mask & (p == v), lane_f, float(LANES)), axis=-1, keepdims=True)
        return v, idx

    mg = lane < N_GROUPS
    p_grp = softmax_on(mg)
    top_pg, top_g = first_max(p_grp, mg)
    lo = EXP_LANE0 + EXPERTS_PER_GROUP * top_g
    me = (lane_f >= lo) & (lane_f < lo + EXPERTS_PER_GROUP)
    p_in = softmax_on(me)
    v1, i1 = first_max(p_in, me)
    me2 = me & (lane_f != i1)
    v2, i2 = first_max(p_in, me2)
    den = v1 + v2
    w0 = top_pg * v1 / den
    w1 = top_pg * v2 / den

    row_ok = (i * tm + _iota((tm, 1), 0)) < n_valid
    oh0 = jnp.where((lane_f == i1) & row_ok, 1.0, 0.0)
    oh1 = jnp.where((lane_f == i2) & row_ok, 1.0, 0.0)
    both = (oh0 + oh1).astype(BF16)
    strict = jnp.where(_iota((tm, tm), 1) < _iota((tm, tm), 0), 1.0, 0.0).astype(BF16)
    before = _dot(strict, both) + carry_ref[...]
    rank0 = jnp.sum(oh0 * before, axis=-1, keepdims=True)
    rank1 = jnp.sum(oh1 * before, axis=-1, keepdims=True)
    carry_ref[...] += jnp.sum(oh0 + oh1, axis=0, keepdims=True)
    cnt_ref[...] = jnp.broadcast_to(carry_ref[...], cnt_ref.shape)

    rec = jnp.zeros(logits.shape, F32)
    for ln, val in ((R_E0, i1 - EXP_LANE0), (R_E1, i2 - EXP_LANE0), (R_W0, w0), (R_W1, w1), (R_RANK0, rank0),
                    (R_RANK1, rank1)):
        rec = jnp.where(lane == ln, val, rec)
    route_ref[...] = rec


def _router(x, gain, w_r, b_r, n_valid):
    m, d = x.shape
    tm = 256
    return pl.pallas_call(
        functools.partial(_router_kernel, n_valid=n_valid),
        grid=(m // tm,),
        in_specs=[pl.BlockSpec((tm, d), lambda i: (i, 0)),
                  pl.BlockSpec((1, d), lambda i: (0, 0)),
                  pl.BlockSpec((d, LANES), lambda i: (0, 0)),
                  pl.BlockSpec((1, LANES), lambda i: (0, 0))],
        out_specs=[pl.BlockSpec((tm, d), lambda i: (i, 0)),
                   pl.BlockSpec((tm, LANES), lambda i: (i, 0)),
                   pl.BlockSpec((SUBLANES, LANES), lambda i: (0, 0))],
        out_shape=[jax.ShapeDtypeStruct((m, d), F32),
                   jax.ShapeDtypeStruct((m, LANES), F32),
                   jax.ShapeDtypeStruct((SUBLANES, LANES), F32)],
        scratch_shapes=[pltpu.VMEM((1, LANES), F32)],
        compiler_params=_cp(1),
        name="moe_router",
    )(x, gain.reshape(1, d), w_r, b_r)


def _moe_gather_kernel(dest_ref, h_hbm, xb_in_hbm, xb_hbm, sem, *, ab):
    del xb_in_hbm
    blk = pl.program_id(0)

    def row_copy(a):
        tok = lax.shift_right_logical(blk * ab + a, 1)
        return pltpu.make_async_copy(h_hbm.at[pl.ds(tok, 1)], xb_hbm.at[pl.ds(dest_ref[0, 0, a], 1)], sem.at[0])

    def start(a, carry):
        row_copy(a).start()
        return carry
    lax.fori_loop(0, ab, start, 0)

    def wait(a, carry):
        row_copy(a).wait()
        return carry
    lax.fori_loop(0, ab, wait, 0)


def _moe_gather(h, dest, n_slots):
    m, d = h.shape
    ab = 512
    na = dest.shape[0]
    assert na % ab == 0
    xb0 = jnp.zeros((n_slots, d), F32)
    return pl.pallas_call(
        functools.partial(_moe_gather_kernel, ab=ab),
        grid=(na // ab,),
        in_specs=[pl.BlockSpec((1, 1, ab), lambda i: (i, 0, 0), memory_space=pltpu.SMEM),
                  pl.BlockSpec(memory_space=pl.ANY),
                  pl.BlockSpec(memory_space=pl.ANY)],
        out_specs=pl.BlockSpec(memory_space=pl.ANY),
        out_shape=jax.ShapeDtypeStruct((n_slots, d), F32),
        scratch_shapes=[pltpu.SemaphoreType.DMA((1,))],
        input_output_aliases={2: 0},
        compiler_params=_cp(1),
        name="moe_gather",
    )(dest.reshape(na // ab, 1, ab), h, xb0)


def _moe_expert_kernel(be_ref, nu_ref, x_ref, wg_ref, wu_ref, wd_ref, y_ref):
    i = pl.program_id(0)

    @pl.when(i < nu_ref[0])
    def _():
        x = x_ref[...].astype(BF16)
        a = _dot(x, wg_ref[...])
        hdn = (a * jax.nn.sigmoid(a)) * _dot(x, wu_ref[...])
        y_ref[...] = _dot(hdn.astype(BF16), wd_ref[...])

    @pl.when(i >= nu_ref[0])
    def _():
        y_ref[...] = jnp.zeros(y_ref.shape, F32)


def _moe_experts(xb, block_e, n_used, w_gate, w_up, w_down):
    n_slots, d = xb.shape
    dff = w_gate.shape[2]
    nblk = n_slots // MOE_ROWS
    grid_spec = pltpu.PrefetchScalarGridSpec(
        num_scalar_prefetch=2,
        grid=(nblk,),
        in_specs=[pl.BlockSpec((MOE_ROWS, d), lambda i, be, nu: (i, 0)),
                  pl.BlockSpec((None, d, dff), lambda i, be, nu: (be[i], 0, 0)),
                  pl.BlockSpec((None, d, dff), lambda i, be, nu: (be[i], 0, 0)),
                  pl.BlockSpec((None, dff, d), lambda i, be, nu: (be[i], 0, 0))],
        out_specs=pl.BlockSpec((MOE_ROWS, d), lambda i, be, nu: (i, 0)),
    )
    return pl.pallas_call(
        _moe_expert_kernel,
        grid_spec=grid_spec,
        out_shape=jax.ShapeDtypeStruct((n_slots, d), F32),
        compiler_params=_cp(1),
        name="moe_experts",
    )(block_e, n_used, xb, w_gate, w_up, w_down)


def _moe_combine_kernel(dest_ref, x_ref, route_ref, g_ref, yb_hbm, o_ref, gbuf, sem, *, tb):
    def row_copy(a):
        return pltpu.make_async_copy(yb_hbm.at[pl.ds(dest_ref[0, 0, a], 1)], gbuf.at[pl.ds(a, 1)], sem.at[0])

    def start(a, carry):
        row_copy(a).start()
        return carry
    lax.fori_loop(0, 2 * tb, start, 0)

    def wait(a, carry):
        row_copy(a).wait()
        return carry
    lax.fori_loop(0, 2 * tb, wait, 0)

    rec = route_ref[...]
    w0 = _lane_pick(rec, R_W0)
    w1 = _lane_pick(rec, R_W1)
    x = x_ref[...] + (gbuf[pl.ds(0, tb), :] * w0 + gbuf[pl.ds(tb, tb), :] * w1)
    o_ref[...] = x * lax.rsqrt(jnp.mean(x * x, axis=-1, keepdims=True) + RMS_EPS) * g_ref[...]


def _moe_combine(x, route, dest2, yb, gain):
    m, d = x.shape
    tb = dest2.shape[2] // 2
    return pl.pallas_call(
        functools.partial(_moe_combine_kernel, tb=tb),
        grid=(m // tb,),
        in_specs=[pl.BlockSpec((1, 1, 2 * tb), lambda i: (i, 0, 0), memory_space=pltpu.SMEM),
                  pl.BlockSpec((tb, d), lambda i: (i, 0)),
                  pl.BlockSpec((tb, LANES), lambda i: (i, 0)),
                  pl.BlockSpec((1, d), lambda i: (0, 0)),
                  pl.BlockSpec(memory_space=pl.ANY)],
        out_specs=pl.BlockSpec((tb, d), lambda i: (i, 0)),
        out_shape=jax.ShapeDtypeStruct((m, d), F32),
        scratch_shapes=[pltpu.VMEM((2 * tb, d), F32), pltpu.SemaphoreType.DMA((1,))],
        compiler_params=_cp(1),
        name="moe_combine",
    )(dest2, x, route, gain.reshape(1, d), yb)


def _pad_lanes(x, width=LANES):
    return jnp.pad(x.reshape(1, -1), ((0, 0), (0, width - x.size)))


def kernel(x_prompt, x_sample, cache_cmp_kv, cache_slc_kv, cache_win_kv, state_gdn, state_gdn_conv, cache_mem_kv, page_table, mem_prompt, norm_mix, w_in, cmp_pe, cmp_w1, cmp_b1, cmp_w2, gdn_conv_w, gdn_a_log, gdn_dt_bias, gdn_norm, w_up_nsa, w_up_gdn, w_out, norm_mem, mem_norm, w_mq, w_mk, w_mv, w_mo, norm_ffn, w_grp, b_grp, w_exp, b_exp, w_gate, w_up, w_down, final_norm):
    bp, s_len, d = x_prompt.shape
    nb, ts, _ = x_sample.shape
    assert bp == 1 and w_in.shape[0] == 1, "one prompt sequence, one layer"
    n_tok = s_len + nb * ts
    t_pad = -(-n_tok // TOK_PAD) * TOK_PAD
    n_pages = page_table.shape[1]
    past = n_pages * PAGE_SIZE
    kv_w = 3 * KV_BRANCH
    c_bg = C_KV + kv_w
    c_sm = c_bg + 2 * d
    assert c_bg % 512 == 0 and c_sm % LANES == 0

    xt = jnp.concatenate([x_prompt[0], x_sample.reshape(nb * ts, d), jnp.zeros((t_pad - n_tok, d), F32)], axis=0)

    sizes = (NSA_WIDTH, kv_w, 3 * NSA_HEADS, GDN_QKV, GDN_HEADS, GDN_HEADS, GDN_WIDTH, 2 * d)
    cuts = np.cumsum(sizes)[:-1].tolist()
    wq, wkv, wg3, wqkv, wa, wb, wz, wbg = jnp.split(w_in[0], cuts, axis=1)
    small = jnp.concatenate([wg3, wa, wb], axis=1)
    small = jnp.pad(small, ((0, 0), (0, LANES - small.shape[1])))
    w_proj = jnp.concatenate([wqkv, wq, wz, wkv, wbg, small], axis=1).astype(BF16)
    proj = _rms_matmul(xt, norm_mix[0], w_proj, 640 if w_proj.shape[1] % 640 == 0 else LANES, F32, "in_proj")

    cmp_w = (cmp_pe[0], cmp_w1[0], cmp_b1[0], cmp_w2[0])
    cmp_p = proj[:s_len, C_KV:C_KV + KV_BRANCH]
    kcvc_p = _compress(cmp_p.reshape(s_len // PAGE_SIZE, PAGE_SIZE, KV_BRANCH),
                       jnp.arange(s_len // PAGE_SIZE, dtype=jnp.int32)[None],
                       jnp.zeros((1, CMP_STRIDE, KV_BRANCH), F32), *cmp_w)
    new_rows = proj[s_len:n_tok, C_KV:C_KV + kv_w].reshape(nb, ts, 3, KV_BRANCH)
    assert ts <= CMP_STRIDE
    tail_s = jnp.pad(new_rows[:, :, 0], ((0, 0), (0, CMP_STRIDE - ts), (0, 0)))
    kcvc_s = _compress(cache_cmp_kv[0].reshape(-1, PAGE_SIZE, KV_BRANCH), page_table, tail_s, *cmp_w)

    o_nsa_p = _nsa_prompt(proj, kcvc_p, s_len, c_sm)
    o_nsa_s = _nsa_sample(proj, kcvc_s, cache_slc_kv[0], cache_win_kv[0], page_table, s_len, ts, c_sm)

    alog, dtb, ng = _pad_lanes(gdn_a_log[0]), _pad_lanes(gdn_dt_bias[0]), gdn_norm[0].reshape(1, GDN_DV)
    o_gdn_p, st_p = _gdn_prompt(proj, s_len, gdn_conv_w[0], alog, dtb, ng, c_sm)
    o_gdn_s, st_s = _gdn_sample(proj, s_len, ts, state_gdn[0], state_gdn_conv[0], gdn_conv_w[0], alog, dtb, ng, c_sm)

    tail_pad = ((0, t_pad - n_tok), (0, 0))
    o_nsa = jnp.pad(jnp.concatenate([o_nsa_p, o_nsa_s.astype(BF16)], axis=0), tail_pad)
    o_gdn = jnp.pad(jnp.concatenate([o_gdn_p, o_gdn_s.astype(BF16)], axis=0), tail_pad)
    mix = _upmix(o_nsa, o_gdn, w_up_nsa[0].astype(BF16), w_up_gdn[0].astype(BF16), proj, c_bg, d)
    x1 = _matmul_res(mix, w_out[0].astype(BF16), xt, 512, "out_proj")

    mem_kv_p = _rms_matmul(mem_prompt[0], mem_norm[0], jnp.concatenate([w_mk[0], w_mv[0]], axis=1).astype(BF16), 512,
                           F32, "mem_kv")
    qm = _rms_matmul(x1, norm_mem[0], w_mq[0].astype(BF16), 512, F32, "mem_q")
    mt = mem_prompt.shape[1]
    om_p = _mem_attn(qm, mem_kv_p.reshape(1, mt, 2 * d), 0, s_len, 512 if s_len % 512 == 0 else Q_BLOCK, False,
                     "mem_attn_prompt")
    om_s = _mem_attn(qm, cache_mem_kv[0].reshape(nb, mt, 2 * d), s_len, nb * ts, ts, True, "mem_attn_sample")
    om = jnp.pad(jnp.concatenate([om_p, om_s], axis=0), tail_pad).astype(BF16)
    x2 = _matmul_res(om, w_mo[0].astype(BF16), x1, 512, "mem_out")

    w_r = jnp.pad(jnp.concatenate([w_grp[0], w_exp[0]], axis=1), ((0, 0), (0, LANES - N_GROUPS - N_EXPERTS)))
    b_r = _pad_lanes(jnp.concatenate([b_grp[0], b_exp[0]]))
    h3, route, cnt = _router(x2, norm_ffn[0], w_r, b_r, n_tok)
    counts = cnt[0, EXP_LANE0:EXP_LANE0 + N_EXPERTS].astype(jnp.int32)
    padded = (counts + MOE_ROWS - 1) // MOE_ROWS * MOE_ROWS
    pad_end = jnp.cumsum(padded)
    pad_start = pad_end - padded
    n_blocks = (TOP_K * n_tok + N_EXPERTS * (MOE_ROWS - 1) + MOE_ROWS - 1) // MOE_ROWS
    dump = n_blocks * MOE_ROWS
    eid = route[:, R_E0:R_E1 + 1].astype(jnp.int32)
    rank = route[:, R_RANK0:R_RANK1 + 1].astype(jnp.int32)
    tok_ok = (jnp.arange(t_pad) < n_tok)[:, None]
    dest = jnp.where(tok_ok, pad_start[jnp.clip(eid, 0, N_EXPERTS - 1)] + rank, dump).astype(jnp.int32)
    block_e = jnp.minimum(jnp.searchsorted(pad_end, jnp.arange(n_blocks + 1) * MOE_ROWS, side="right"),
                          N_EXPERTS - 1).astype(jnp.int32)
    n_used = (pad_end[-1:] // MOE_ROWS).astype(jnp.int32)
    xb = _moe_gather(h3, dest.reshape(-1), (n_blocks + 1) * MOE_ROWS)
    yb = _moe_experts(xb, block_e, n_used, w_gate[0].astype(BF16), w_up[0].astype(BF16), w_down[0].astype(BF16))
    tbc = 128
    dest2 = dest.reshape(t_pad // tbc, tbc, 2).transpose(0, 2, 1).reshape(t_pad // tbc, 1, 2 * tbc)
    y = _moe_combine(x2, route, dest2, yb, final_norm)

    y_prompt = y[:s_len].reshape(1, s_len, d)
    y_sample = y[s_len:n_tok].reshape(nb, ts, d)
    kv_p = proj[:s_len, C_KV:C_KV + kv_w].reshape(1, 1, s_len, 3, NSA_KV_HEADS, 2, HEAD_DIM)
    wlen = min(WINDOW, s_len)
    new_rows = new_rows.reshape(1, nb, ts, 3, NSA_KV_HEADS, 2, HEAD_DIM)
    wb = cache_win_kv.shape[2]
    new_win_s = jnp.concatenate([cache_win_kv, new_rows[:, :, :, 2]], axis=2)[:, :, -wb:]
    conv_p = proj[s_len - (CONV_W - 1):s_len, C_QKV:C_QKV + GDN_QKV].reshape(1, 1, CONV_W - 1, GDN_QKV)
    qkv_s = proj[s_len:n_tok, C_QKV:C_QKV + GDN_QKV].reshape(nb, ts, GDN_QKV)
    conv_s = jnp.concatenate([state_gdn_conv[0], qkv_s], axis=1)[:, -(CONV_W - 1):][None]
    return (y_prompt, y_sample,
            kv_p[:, :, :, 0], kv_p[:, :, :, 1], kv_p[:, :, s_len - wlen:, 2],
            st_p[None, None], conv_p, mem_kv_p.reshape(1, 1, mt, 2, MEM_HEADS, d // MEM_HEADS),
            new_rows[:, :, :, 0], new_rows[:, :, :, 1], new_win_s, st_s[None], conv_s)
```
